```python
import jax, jax.numpy as jnp
from jax import lax
import numpy as np

D_MODEL = 1024
BATCH = 8
SEQ = 4096
DEPTH = 2

CHUNK = 128
A_GROUPS = 4
A_GROUP_DIM = 128
A_WIDTH = A_GROUPS * A_GROUP_DIM
DIL_WINDOWS = (128, 512, 2048)
DIL_RATES = (1, 4, 16)
N_DIL = 3
B_HEADS = 8
B_HEAD_DIM = 64
B_WIDTH = B_HEADS * B_HEAD_DIM
ROPE_DIM = B_HEAD_DIM // 4
ROPE_THETA = 500000.0
C_WIDTH = 512
C_KERNEL = 31
D_WIDTH = 512
D_KERNEL = 3
D_FF = -(-8 * D_MODEL // (3 * 256)) * 256
EPS = 1e-6
NEG_INF = -1e30

AB_IN = 2 * A_WIDTH + 3 * N_DIL * B_HEADS * B_HEAD_DIM
AB_OUT = A_WIDTH + B_WIDTH
CD_IN = 2 * C_WIDTH + 3 * D_WIDTH
CD_OUT = C_WIDTH + D_WIDTH
N_AB = (DEPTH + 1) // 2
N_CD = DEPTH // 2

kernel_name = 'hybrid_gmlp_dilattn_conformer_shortconv'


def rms_norm(x, g):
    xf = x.astype(jnp.float32)
    y = xf * lax.rsqrt(jnp.mean(xf * xf, axis=-1, keepdims=True) + EPS)
    return (y * g.astype(jnp.float32)).astype(x.dtype)


def layer_norm(x, g, b):
    xf = x.astype(jnp.float32)
    xc = xf - jnp.mean(xf, axis=-1, keepdims=True)
    y = xc * lax.rsqrt(jnp.mean(xc * xc, axis=-1, keepdims=True) + EPS)
    return (y * g.astype(jnp.float32) + b.astype(jnp.float32)).astype(x.dtype)


def apply_rope(x, pos):
    half = ROPE_DIM // 2
    inv_freq = ROPE_THETA ** (-jnp.arange(half, dtype=jnp.float32) * 2.0 / ROPE_DIM)
    ang = pos[:, None] * inv_freq[None, :]
    cos = jnp.cos(ang)[None, :, None, :]
    sin = jnp.sin(ang)[None, :, None, :]
    xf = x.astype(jnp.float32)
    x1 = xf[..., :half]
    x2 = xf[..., half:ROPE_DIM]
    out = jnp.concatenate([x1 * cos - x2 * sin, x1 * sin + x2 * cos, xf[..., ROPE_DIM:]], axis=-1)
    return out.astype(x.dtype)


def causal_depthwise_conv(x, w):
    k, c = w.shape
    return lax.conv_general_dilated(
        x, w.astype(x.dtype)[:, None, :], window_strides=(1,), padding=[(k - 1, 0)],
        dimension_numbers=('NWC', 'WIO', 'NWC'), feature_group_count=c)


def chunked_spatial_gating(u, v, w_s, b_s):
    bn, s, _ = u.shape
    nc = s // CHUNK
    vg = v.reshape(bn, nc, CHUNK, A_GROUPS, A_GROUP_DIM)
    causal = jnp.tril(jnp.ones((CHUNK, CHUNK), dtype=bool))
    w = jnp.where(causal[None], w_s, 0.0).astype(v.dtype)
    mixed = jnp.einsum('gts,bnsgc->bntgc', w, vg) + b_s.T.astype(v.dtype)[None, None, :, :, None]
    return u * mixed.reshape(bn, s, A_WIDTH)


def dilated_group_attention(q, k, v, rate, n_back):
    bn, s, h, dh = q.shape
    span = rate * n_back
    s_pad = -(-s // span) * span
    length = s_pad // rate
    nb = length // n_back

    def to_blocks(t):
        t = jnp.pad(t, ((0, 0), (0, s_pad - s), (0, 0), (0, 0)))
        t = t.reshape(bn, length, rate, h, dh).transpose(0, 2, 1, 3, 4)
        return t.reshape(bn, rate, nb, n_back, h, dh)

    def with_prev(t):
        prev = jnp.pad(t[:, :, :-1], ((0, 0), (0, 0), (1, 0), (0, 0), (0, 0), (0, 0)))
        return jnp.concatenate([prev, t], axis=3)

    qb = to_blocks(q)
    kk = with_prev(to_blocks(k))
    vv = with_prev(to_blocks(v))
    scores = jnp.einsum('brnqhd,brnkhd->brnhqk', qb.astype(jnp.float32),
                        kk.astype(jnp.float32)) * (dh ** -0.5)
    qi = jnp.arange(n_back)[:, None]
    kj = jnp.arange(2 * n_back)[None, :]
    dist = qi + n_back - kj
    band = (dist >= 0) & (dist <= n_back)
    exists = (jnp.arange(nb)[:, None, None] > 0) | (kj[None] >= n_back)
    mask = band[None] & exists
    scores = jnp.where(mask[None, None, :, None], scores, NEG_INF)
    lse = jax.nn.logsumexp(scores, axis=-1)
    probs = jnp.exp(scores - lse[..., None])
    out = jnp.einsum('brnhqk,brnkhd->brnqhd', probs, vv.astype(jnp.float32))
    out = out.reshape(bn, rate, length, h, dh).transpose(0, 2, 1, 3, 4).reshape(bn, s_pad, h, dh)[:, :s]
    lse = lse.transpose(0, 1, 2, 4, 3).reshape(bn, rate, length, h)
    lse = lse.transpose(0, 2, 1, 3).reshape(bn, s_pad, h)[:, :s]
    return out, lse


def mixer_ab(h, w_in, sgu_g, sgu_b, sgu_w, sgu_bias, q_g, k_g, w_out):
    bn, s, _ = h.shape
    proj = h @ w_in
    a = jax.nn.gelu(proj[..., :2 * A_WIDTH], approximate=False)
    u, v = a[..., :A_WIDTH], a[..., A_WIDTH:]
    a_out = chunked_spatial_gating(u, layer_norm(v, sgu_g, sgu_b), sgu_w, sgu_bias)
    qkv = proj[..., 2 * A_WIDTH:].reshape(bn, s, 3, N_DIL, B_HEADS, B_HEAD_DIM)
    pos = jnp.arange(s, dtype=jnp.float32)
    outs, lses = [], []
    for g in range(N_DIL):
        q = apply_rope(rms_norm(qkv[:, :, 0, g], q_g[g]), pos)
        k = apply_rope(rms_norm(qkv[:, :, 1, g], k_g[g]), pos)
        o, l = dilated_group_attention(q, k, qkv[:, :, 2, g], DIL_RATES[g], DIL_WINDOWS[g] // DIL_RATES[g])
        outs.append(o)
        lses.append(l)
    wts = jax.nn.softmax(jnp.stack(lses), axis=0)
    b_out = jnp.sum(wts[..., None] * jnp.stack(outs), axis=0).astype(h.dtype).reshape(bn, s, B_WIDTH)
    return jnp.concatenate([a_out, b_out], axis=-1) @ w_out


def mixer_cd(h, w_in, conv_c_w, conv_c_b, c_ln_g, c_ln_b, conv_d_w, w_out):
    proj = h @ w_in
    c = proj[..., :C_WIDTH] * jax.nn.sigmoid(proj[..., C_WIDTH:2 * C_WIDTH])
    c = causal_depthwise_conv(c, conv_c_w) + conv_c_b.astype(c.dtype)
    c = jax.nn.silu(layer_norm(c, c_ln_g, c_ln_b))
    gate_b, gate_c, hv = jnp.split(proj[..., 2 * C_WIDTH:], 3, axis=-1)
    d = gate_b * causal_depthwise_conv(gate_c * hv, conv_d_w)
    return jnp.concatenate([c, d], axis=-1) @ w_out


def swiglu(h, w_gate, w_up, w_down):
    return (jax.nn.silu(h @ w_gate) * (h @ w_up)) @ w_down


def setup_inputs(seed: int = 0) -> dict:
    key = jax.random.key(seed)
    ks = jax.random.split(key, 24)
    f32 = jnp.float32

    def nrm(k, shape, scale):
        return jax.random.normal(k, shape, f32) * scale

    def gain(k, shape):
        return 1.0 + 0.05 * jax.random.normal(k, shape, f32)

    return {
        'x': jax.random.normal(ks[0], (BATCH, SEQ, D_MODEL), f32),
        'ab_norm_g': gain(ks[1], (N_AB, D_MODEL)),
        'ab_w_in': nrm(ks[2], (N_AB, D_MODEL, AB_IN), D_MODEL ** -0.5),
        'sgu_norm_g': gain(ks[3], (N_AB, A_WIDTH)),
        'sgu_norm_b': nrm(ks[4], (N_AB, A_WIDTH), 0.02),
        'sgu_w': nrm(ks[5], (N_AB, A_GROUPS, CHUNK, CHUNK), 0.5 * CHUNK ** -0.5),
        'sgu_bias': gain(ks[6], (N_AB, A_GROUPS, CHUNK)),
        'q_norm_g': gain(ks[7], (N_AB, N_DIL, B_HEAD_DIM)),
        'k_norm_g': gain(ks[8], (N_AB, N_DIL, B_HEAD_DIM)),
        'ab_w_out': nrm(ks[9], (N_AB, AB_OUT, D_MODEL), AB_OUT ** -0.5),
        'cd_norm_g': gain(ks[10], (N_CD, D_MODEL)),
        'cd_w_in': nrm(ks[11], (N_CD, D_MODEL, CD_IN), D_MODEL ** -0.5),
        'conv_c_w': nrm(ks[12], (N_CD, C_KERNEL, C_WIDTH), C_KERNEL ** -0.5),
        'conv_c_b': nrm(ks[13], (N_CD, C_WIDTH), 0.02),
        'c_ln_g': gain(ks[14], (N_CD, C_WIDTH)),
        'c_ln_b': nrm(ks[15], (N_CD, C_WIDTH), 0.02),
        'conv_d_w': nrm(ks[16], (N_CD, D_KERNEL, D_WIDTH), D_KERNEL ** -0.5),
        'cd_w_out': nrm(ks[17], (N_CD, CD_OUT, D_MODEL), CD_OUT ** -0.5),
        'ffn_norm_g': gain(ks[18], (DEPTH, D_MODEL)),
        'ffn_w_gate': nrm(ks[19], (DEPTH, D_MODEL, D_FF), D_MODEL ** -0.5),
        'ffn_w_up': nrm(ks[20], (DEPTH, D_MODEL, D_FF), D_MODEL ** -0.5),
        'ffn_w_down': nrm(ks[21], (DEPTH, D_FF, D_MODEL), D_FF ** -0.5),
    }


def reference(x, ab_norm_g, ab_w_in, sgu_norm_g, sgu_norm_b, sgu_w, sgu_bias, q_norm_g, k_norm_g,
              ab_w_out, cd_norm_g, cd_w_in, conv_c_w, conv_c_b, c_ln_g, c_ln_b, conv_d_w, cd_w_out,
              ffn_norm_g, ffn_w_gate, ffn_w_up, ffn_w_down):
    for layer in range(DEPTH):
        i = layer // 2
        if layer % 2 == 0:
            h = rms_norm(x, ab_norm_g[i])
            x = x + mixer_ab(h, ab_w_in[i], sgu_norm_g[i], sgu_norm_b[i], sgu_w[i], sgu_bias[i],
                             q_norm_g[i], k_norm_g[i], ab_w_out[i])
        else:
            h = rms_norm(x, cd_norm_g[i])
            x = x + mixer_cd(h, cd_w_in[i], conv_c_w[i], conv_c_b[i], c_ln_g[i], c_ln_b[i],
                             conv_d_w[i], cd_w_out[i])
        h = rms_norm(x, ffn_norm_g[layer])
        x = x + swiglu(h, ffn_w_gate[layer], ffn_w_up[layer], ffn_w_down[layer])
    return x
```

```python
import functools

import numpy as np
import jax
import jax.numpy as jnp
from jax import lax
from jax.experimental import pallas as pl
from jax.experimental.pallas import tpu as pltpu

F32 = jnp.float32
BF16 = jnp.bfloat16

LANES = 128
CHUNK = 128
A_GROUPS = 4
A_WIDTH = 512
DIL_RATES = (1, 4, 16)
N_BACK = 128
N_DIL = 3
HEAD_DIM = 64
B_WIDTH = 512
ROPE_DIM = 16
ROPE_THETA = 500000.0
C_WIDTH = 512
C_KERNEL = 31
D_WIDTH = 512
D_KERNEL = 3
EPS = 1e-6
NEG_INF = -1e30
SQRT_HALF = float(np.sqrt(0.5))

TOKEN_TILE = 256
ATTN_Q_ROWS = 256
C_HALO = 32
D_HALO = 8
VMEM_LIMIT = 56 * 1024 * 1024


def _const_spec(shape):
    nd = len(shape)
    return pl.BlockSpec(shape, lambda *_: (0,) * nd, pipeline_mode=pl.Buffered(1))


def _rms(x, g):
    return x * lax.rsqrt(jnp.mean(x * x, axis=-1, keepdims=True) + EPS) * g


def _layer_norm(x, g, b):
    xc = x - jnp.mean(x, axis=-1, keepdims=True)
    return xc * lax.rsqrt(jnp.mean(xc * xc, axis=-1, keepdims=True) + EPS) * g + b


def _sigmoid(x):
    return 1.0 / (1.0 + jnp.exp(-x))


def _swiglu_residual(x1, g, wg_ref, wu_ref, wd_ref):
    h = _rms(x1, g).astype(BF16)
    gate = jnp.dot(h, wg_ref[...], preferred_element_type=F32)
    up = jnp.dot(h, wu_ref[...], preferred_element_type=F32)
    act = (gate * _sigmoid(gate) * up).astype(BF16)
    return x1 + jnp.dot(act, wd_ref[...], preferred_element_type=F32)


def _front_ab_kernel(x_ref, g_ref, w_ref, sgu_g_ref, sgu_b_ref, sgu_w_ref, sgu_bias_ref,
                     qg_ref, kg_ref, cos_ref, sin_lo_ref, sin_hi_ref, headmean_ref,
                     a_ref, qkv0_ref, qkv1_ref, qkv2_ref, stage_ref):
    tm = x_ref.shape[0]
    nchunk = tm // CHUNK
    h = _rms(x_ref[...], g_ref[...]).astype(BF16)

    pa = jnp.dot(h, w_ref[:, 0:2 * A_WIDTH], preferred_element_type=F32)
    act = 0.5 * pa * (1.0 + lax.erf(pa * SQRT_HALF))
    u = act[:, :A_WIDTH]
    vn = _layer_norm(act[:, A_WIDTH:], sgu_g_ref[...], sgu_b_ref[...]).astype(BF16)
    mixed = []
    for g in range(A_GROUPS):
        rhs = jnp.concatenate(
            [vn[c * CHUNK:(c + 1) * CHUNK, g * LANES:(g + 1) * LANES] for c in range(nchunk)], axis=1)
        mixed.append(jnp.dot(sgu_w_ref[g], rhs, preferred_element_type=F32))
    mixed = jnp.concatenate(
        [jnp.concatenate([mixed[g][:, c * LANES:(c + 1) * LANES] for g in range(A_GROUPS)], axis=1)
         for c in range(nchunk)], axis=0)
    bias = jnp.concatenate([sgu_bias_ref[...]] * nchunk, axis=0)
    a_ref[...] = (u * (mixed + bias)).astype(BF16)

    cos = cos_ref[...]
    sin_lo = sin_lo_ref[...]
    sin_hi = sin_hi_ref[...]

    def rope(y):
        parts = []
        for j in range(B_WIDTH // LANES):
            yj = y[:, j * LANES:(j + 1) * LANES]
            nxt = pltpu.roll(yj, LANES - ROPE_DIM // 2, axis=1)
            prv = pltpu.roll(yj, ROPE_DIM // 2, axis=1)
            parts.append(yj * cos + nxt * sin_lo + prv * sin_hi)
        return jnp.concatenate(parts, axis=1)

    def store(out_ref, rate, piece, y):
        cols = slice(piece * B_WIDTH, (piece + 1) * B_WIDTH)
        if rate == 1:
            out_ref[0, :, cols] = y.astype(BF16)
            return
        for j in range(B_WIDTH // LANES):
            stage_ref[j] = y[:, j * LANES:(j + 1) * LANES]
        for c in range(rate):
            rows = jnp.concatenate(
                [stage_ref[j, pl.ds(c, tm // rate, stride=rate), :] for j in range(B_WIDTH // LANES)],
                axis=1)
            out_ref[c, :, cols] = rows.astype(BF16)

    base = 2 * A_WIDTH
    for grp, (rate, out_ref) in enumerate(zip(DIL_RATES, (qkv0_ref, qkv1_ref, qkv2_ref))):
        for piece, gain_ref in ((0, qg_ref), (1, kg_ref)):
            col0 = base + (piece * N_DIL + grp) * B_WIDTH
            p = jnp.dot(h, w_ref[:, col0:col0 + B_WIDTH], preferred_element_type=F32)
            ms = jnp.dot((p * p).astype(BF16), headmean_ref[...], preferred_element_type=F32)
            y = rope(p * lax.rsqrt(ms + EPS) * gain_ref[grp])
            if piece == 0:
                y = y * (HEAD_DIM ** -0.5)
            store(out_ref, rate, piece, y)
        col0 = base + (2 * N_DIL + grp) * B_WIDTH
        store(out_ref, rate, 2, jnp.dot(h, w_ref[:, col0:col0 + B_WIDTH], preferred_element_type=F32))


def _front_ab(x, g, w_in, sgu_g, sgu_b, sgu_w, sgu_bias, q_gain, k_gain, cos, sin_lo, sin_hi, headmean, tm):
    bsz, seq, d = x.shape
    nt = seq // tm
    tile = lambda b, i: (b, i, 0)
    row = lambda b, i: (i, 0)
    qkv_shapes = [jax.ShapeDtypeStruct((bsz, r, seq // r, 3 * B_WIDTH), BF16) for r in DIL_RATES]
    qkv_specs = [pl.BlockSpec((None, r, tm // r, 3 * B_WIDTH), lambda b, i: (b, 0, i, 0)) for r in DIL_RATES]
    return pl.pallas_call(
        _front_ab_kernel,
        grid=(bsz, nt),
        in_specs=[
            pl.BlockSpec((None, tm, d), tile),
            _const_spec(g.shape), _const_spec(w_in.shape),
            _const_spec(sgu_g.shape), _const_spec(sgu_b.shape), _const_spec(sgu_w.shape),
            _const_spec(sgu_bias.shape), _const_spec(q_gain.shape), _const_spec(k_gain.shape),
            pl.BlockSpec((tm, LANES), row), pl.BlockSpec((tm, LANES), row), pl.BlockSpec((tm, LANES), row),
            _const_spec(headmean.shape),
        ],
        out_specs=[pl.BlockSpec((None, tm, A_WIDTH), tile)] + qkv_specs,
        out_shape=[jax.ShapeDtypeStruct((bsz, seq, A_WIDTH), BF16)] + qkv_shapes,
        scratch_shapes=[pltpu.VMEM((B_WIDTH // LANES, tm, LANES), F32)],
        compiler_params=pltpu.CompilerParams(
            dimension_semantics=("arbitrary", "arbitrary"), vmem_limit_bytes=VMEM_LIMIT),
        name="front_ab",
    )(x, g, w_in, sgu_g, sgu_b, sgu_w, sgu_bias, q_gain, k_gain, cos, sin_lo, sin_hi, headmean)


def _dil_attn_kernel(qkv_ref, bias_ref, o_ref, lse_ref, k_ref, v_ref):
    qb = qkv_ref.shape[0]
    j = pl.program_id(2)

    @pl.when(j == 0)
    def _():
        k_ref[0:N_BACK, :] = jnp.zeros((N_BACK, B_WIDTH), BF16)
        v_ref[0:N_BACK, :] = jnp.zeros((N_BACK, B_WIDTH), BF16)

    k_ref[N_BACK:N_BACK + qb, :] = qkv_ref[:, B_WIDTH:2 * B_WIDTH]
    v_ref[N_BACK:N_BACK + qb, :] = qkv_ref[:, 2 * B_WIDTH:3 * B_WIDTH]

    lane = lax.broadcasted_iota(jnp.int32, (N_BACK, LANES), 1)
    low_head = lane < HEAD_DIM
    has_prev = jnp.where(j == 0, 0, 1)

    for s in range(qb // N_BACK):
        rows = slice(s * N_BACK, (s + 1) * N_BACK)
        bias = bias_ref[1] if s > 0 else bias_ref[has_prev]
        for hp in range(B_WIDTH // LANES):
            cols = slice(hp * LANES, (hp + 1) * LANES)
            q2 = qkv_ref[rows, cols]
            k2 = k_ref[s * N_BACK:(s + 2) * N_BACK, cols]
            v2 = v_ref[s * N_BACK:(s + 2) * N_BACK, cols]
            outs, lses = [], []
            for sel in (low_head, jnp.logical_not(low_head)):
                qm = jnp.where(sel, q2, jnp.zeros_like(q2))
                sc = lax.dot_general(qm, k2, (((1,), (1,)), ((), ())), preferred_element_type=F32) + bias
                m = jnp.max(sc, axis=-1, keepdims=True)
                p = jnp.exp(sc - m)
                l = jnp.sum(p, axis=-1, keepdims=True)
                outs.append(jnp.dot(p.astype(BF16), v2, preferred_element_type=F32) / l)
                lses.append(m + jnp.log(l))
            o_ref[rows, cols] = jnp.where(low_head, outs[0], outs[1]).astype(BF16)
            lse_ref[rows, cols] = jnp.where(low_head, lses[0], lses[1])

    k_ref[0:N_BACK, :] = k_ref[qb:qb + N_BACK, :]
    v_ref[0:N_BACK, :] = v_ref[qb:qb + N_BACK, :]


def _dil_attn(qkv, bias, qb):
    bsz, rate, length, _ = qkv.shape
    blk = lambda b, c, j: (b, c, j, 0)
    return pl.pallas_call(
        _dil_attn_kernel,
        grid=(bsz, rate, length // qb),
        in_specs=[pl.BlockSpec((None, None, qb, 3 * B_WIDTH), blk), _const_spec(bias.shape)],
        out_specs=[pl.BlockSpec((None, None, qb, B_WIDTH), blk), pl.BlockSpec((None, None, qb, B_WIDTH), blk)],
        out_shape=[jax.ShapeDtypeStruct((bsz, rate, length, B_WIDTH), BF16),
                   jax.ShapeDtypeStruct((bsz, rate, length, B_WIDTH), F32)],
        scratch_shapes=[pltpu.VMEM((N_BACK + qb, B_WIDTH), BF16), pltpu.VMEM((N_BACK + qb, B_WIDTH), BF16)],
        compiler_params=pltpu.CompilerParams(
            dimension_semantics=("arbitrary", "arbitrary", "arbitrary"), vmem_limit_bytes=VMEM_LIMIT),
        name=f"dil_attn_r{rate}",
    )(qkv, bias)


def _ab_out_ffn_kernel(x_ref, a_ref, o0_ref, o1_ref, o2_ref, l0_ref, l1_ref, l2_ref,
                       wout_ref, fg_ref, wg_ref, wu_ref, wd_ref, out_ref, nat_ref):
    tm = x_ref.shape[0]

    def natural(ref, rate):
        if rate == 1:
            return ref[0].astype(F32)
        for c in range(rate):
            val = ref[c].astype(F32)
            for j in range(B_WIDTH // LANES):
                nat_ref[j, pl.ds(c, tm // rate, stride=rate), :] = val[:, j * LANES:(j + 1) * LANES]
        return jnp.concatenate([nat_ref[j] for j in range(B_WIDTH // LANES)], axis=1)

    lse = [natural(r, rate) for r, rate in zip((l0_ref, l1_ref, l2_ref), DIL_RATES)]
    m = jnp.maximum(jnp.maximum(lse[0], lse[1]), lse[2])
    e = [jnp.exp(l - m) for l in lse]
    den = e[0] + e[1] + e[2]
    num = e[0] * natural(o0_ref, DIL_RATES[0])
    num = num + e[1] * natural(o1_ref, DIL_RATES[1])
    num = num + e[2] * natural(o2_ref, DIL_RATES[2])
    mix = jnp.concatenate([a_ref[...], (num / den).astype(BF16)], axis=1)
    x1 = x_ref[...] + jnp.dot(mix, wout_ref[...], preferred_element_type=F32)
    out_ref[...] = _swiglu_residual(x1, fg_ref[...], wg_ref, wu_ref, wd_ref)


def _ab_out_ffn(x, a, outs, lses, w_out, fg, wg, wu, wd, tm):
    bsz, seq, d = x.shape
    tile = lambda b, i: (b, i, 0)
    cls = lambda b, i: (b, 0, i, 0)
    cls_specs = [pl.BlockSpec((None, r, tm // r, B_WIDTH), cls) for r in DIL_RATES]
    return pl.pallas_call(
        _ab_out_ffn_kernel,
        grid=(bsz, seq // tm),
        in_specs=[pl.BlockSpec((None, tm, d), tile), pl.BlockSpec((None, tm, A_WIDTH), tile)]
        + cls_specs + cls_specs
        + [_const_spec(w_out.shape), _const_spec(fg.shape), _const_spec(wg.shape),
           _const_spec(wu.shape), _const_spec(wd.shape)],
        out_specs=pl.BlockSpec((None, tm, d), tile),
        out_shape=jax.ShapeDtypeStruct(x.shape, F32),
        scratch_shapes=[pltpu.VMEM((B_WIDTH // LANES, tm, LANES), F32)],
        compiler_params=pltpu.CompilerParams(
            dimension_semantics=("arbitrary", "arbitrary"), vmem_limit_bytes=VMEM_LIMIT),
        name="ab_out_ffn",
    )(x, a, *outs, *lses, w_out, fg, wg, wu, wd)


def _layer_cd_kernel(x_ref, g_ref, win_ref, cw_ref, cb_ref, lng_ref, lnb_ref, dw_ref,
                     wout_ref, fg_ref, wg_ref, wu_ref, wd_ref, out_ref, chist_ref, dhist_ref):
    tm = x_ref.shape[0]

    @pl.when(pl.program_id(1) == 0)
    def _():
        chist_ref[0:C_HALO, :] = jnp.zeros((C_HALO, C_WIDTH), F32)
        dhist_ref[0:D_HALO, :] = jnp.zeros((D_HALO, D_WIDTH), F32)

    x = x_ref[...]
    h = _rms(x, g_ref[...]).astype(BF16)

    pc = jnp.dot(h, win_ref[:, 0:2 * C_WIDTH], preferred_element_type=F32)
    chist_ref[C_HALO:C_HALO + tm, :] = pc[:, :C_WIDTH] * _sigmoid(pc[:, C_WIDTH:])
    acc = jnp.zeros((tm, C_WIDTH), F32) + cb_ref[...]
    for k in range(C_KERNEL):
        off = C_HALO - (C_KERNEL - 1) + k
        acc = acc + chist_ref[off:off + tm, :] * cw_ref[k:k + 1, :]
    chist_ref[0:C_HALO, :] = chist_ref[tm:tm + C_HALO, :]
    cn = _layer_norm(acc, lng_ref[...], lnb_ref[...])
    c_out = (cn * _sigmoid(cn)).astype(BF16)

    pd = jnp.dot(h, win_ref[:, 2 * C_WIDTH:], preferred_element_type=F32)
    dhist_ref[D_HALO:D_HALO + tm, :] = pd[:, D_WIDTH:2 * D_WIDTH] * pd[:, 2 * D_WIDTH:]
    dacc = jnp.zeros((tm, D_WIDTH), F32)
    for k in range(D_KERNEL):
        off = D_HALO - (D_KERNEL - 1) + k
        dacc = dacc + dhist_ref[off:off + tm, :] * dw_ref[k:k + 1, :]
    dhist_ref[0:D_HALO, :] = dhist_ref[tm:tm + D_HALO, :]
    d_out = (pd[:, :D_WIDTH] * dacc).astype(BF16)

    mix = jnp.concatenate([c_out, d_out], axis=1)
    x1 = x + jnp.dot(mix, wout_ref[...], preferred_element_type=F32)
    out_ref[...] = _swiglu_residual(x1, fg_ref[...], wg_ref, wu_ref, wd_ref)


def _layer_cd(x, g, w_in, cw, cb, lng, lnb, dw, w_out, fg, wg, wu, wd, tm):
    bsz, seq, d = x.shape
    tile = lambda b, i: (b, i, 0)
    consts = (g, w_in, cw, cb, lng, lnb, dw, w_out, fg, wg, wu, wd)
    return pl.pallas_call(
        _layer_cd_kernel,
        grid=(bsz, seq // tm),
        in_specs=[pl.BlockSpec((None, tm, d), tile)] + [_const_spec(c.shape) for c in consts],
        out_specs=pl.BlockSpec((None, tm, d), tile),
        out_shape=jax.ShapeDtypeStruct(x.shape, F32),
        scratch_shapes=[pltpu.VMEM((C_HALO + tm, C_WIDTH), F32), pltpu.VMEM((D_HALO + tm, D_WIDTH), F32)],
        compiler_params=pltpu.CompilerParams(
            dimension_semantics=("arbitrary", "arbitrary"), vmem_limit_bytes=VMEM_LIMIT),
        name="layer_cd",
    )(x, *consts)


def _rope_tables(seq):
    half = ROPE_DIM // 2
    pos = jnp.arange(seq, dtype=F32)
    inv_freq = ROPE_THETA ** (-jnp.arange(half, dtype=F32) * 2.0 / ROPE_DIM)
    ang = pos[:, None] * inv_freq[None, :]
    cos, sin = jnp.cos(ang), jnp.sin(ang)
    ones = jnp.ones((seq, HEAD_DIM - ROPE_DIM), F32)
    zeros = jnp.zeros((seq, HEAD_DIM - ROPE_DIM), F32)
    zh = jnp.zeros((seq, half), F32)
    cos_t = jnp.concatenate([cos, cos, ones], axis=1)
    sin_lo = jnp.concatenate([-sin, zh, zeros], axis=1)
    sin_hi = jnp.concatenate([zh, sin, zeros], axis=1)
    return tuple(jnp.concatenate([t, t], axis=1) for t in (cos_t, sin_lo, sin_hi))


def _attn_bias():
    qi = np.arange(N_BACK)[:, None]
    kj = np.arange(2 * N_BACK)[None, :]
    dist = qi + N_BACK - kj
    band = (dist >= 0) & (dist <= N_BACK)
    masks = np.stack([band & (kj >= N_BACK), band])
    return jnp.asarray(np.where(masks, 0.0, NEG_INF), dtype=F32)


def kernel(x, ab_norm_g, ab_w_in, sgu_norm_g, sgu_norm_b, sgu_w, sgu_bias, q_norm_g, k_norm_g, ab_w_out,
           cd_norm_g, cd_w_in, conv_c_w, conv_c_b, c_ln_g, c_ln_b, conv_d_w, cd_w_out,
           ffn_norm_g, ffn_w_gate, ffn_w_up, ffn_w_down):
    bsz, seq, d = x.shape
    tm = TOKEN_TILE
    row = lambda v: v.reshape(1, -1)

    causal = jnp.tril(jnp.ones((CHUNK, CHUNK), dtype=bool))
    sgu_w_m = jnp.where(causal[None], sgu_w[0], 0.0).astype(BF16)
    sgu_bias_t = jnp.repeat(sgu_bias[0].T, A_WIDTH // A_GROUPS, axis=1)
    tile_heads = lambda gain: jnp.tile(gain, (1, B_WIDTH // HEAD_DIM)).reshape(N_DIL, 1, B_WIDTH)
    head_id = np.arange(B_WIDTH) // HEAD_DIM
    headmean = jnp.asarray((head_id[:, None] == head_id[None, :]) / HEAD_DIM, dtype=BF16)
    cos, sin_lo, sin_hi = _rope_tables(seq)

    a_out, qkv0, qkv1, qkv2 = _front_ab(
        x, row(ab_norm_g[0]), ab_w_in[0].astype(BF16), row(sgu_norm_g[0]), row(sgu_norm_b[0]), sgu_w_m,
        sgu_bias_t, tile_heads(q_norm_g[0]), tile_heads(k_norm_g[0]), cos, sin_lo, sin_hi, headmean, tm)

    bias = _attn_bias()
    outs, lses = [], []
    for qkv in (qkv0, qkv1, qkv2):
        o, l = _dil_attn(qkv, bias, min(ATTN_Q_ROWS, qkv.shape[2]))
        outs.append(o)
        lses.append(l)

    x = _ab_out_ffn(x, a_out, outs, lses, ab_w_out[0].astype(BF16), row(ffn_norm_g[0]),
                    ffn_w_gate[0].astype(BF16), ffn_w_up[0].astype(BF16), ffn_w_down[0].astype(BF16), tm)

    x = _layer_cd(x, row(cd_norm_g[0]), cd_w_in[0].astype(BF16), conv_c_w[0], row(conv_c_b[0]),
                  row(c_ln_g[0]), row(c_ln_b[0]), conv_d_w[0], cd_w_out[0].astype(BF16),
                  row(ffn_norm_g[1]), ffn_w_gate[1].astype(BF16), ffn_w_up[1].astype(BF16),
                  ffn_w_down[1].astype(BF16), tm)
    return x
```

```python
import numpy as np
import jax
import jax.numpy as jnp
from jax import lax
from jax.experimental import pallas as pl
from jax.experimental.pallas import tpu as pltpu

F32 = jnp.float32
BF16 = jnp.bfloat16

LANES = 128
CHUNK = 128
A_GROUPS = 4
A_WIDTH = 512
DIL_RATES = (1, 4, 16)
N_BACK = 128
N_DIL = 3
HEAD_DIM = 64
B_WIDTH = 512
ROPE_DIM = 16
ROPE_THETA = 500000.0
C_WIDTH = 512
C_KERNEL = 31
D_WIDTH = 512
D_KERNEL = 3
EPS = 1e-6
NEG_INF = -1e30
SQRT_HALF = float(np.sqrt(0.5))
LOG2_E = float(np.log2(np.e))
LN_2 = float(np.log(2.0))
Q_SCALE = HEAD_DIM ** -0.5 * LOG2_E

TOKEN_TILE = 256
ATTN_Q_ROWS = 256
C_HALO = 32
D_HALO = 8
VMEM_LIMIT = 56 * 1024 * 1024


def _const_spec(shape):
    nd = len(shape)
    return pl.BlockSpec(shape, lambda *_: (0,) * nd, pipeline_mode=pl.Buffered(1))


def _rms(x, g):
    return x * lax.rsqrt(jnp.mean(x * x, axis=-1, keepdims=True) + EPS) * g


def _layer_norm(x, g, b):
    xc = x - jnp.mean(x, axis=-1, keepdims=True)
    return xc * lax.rsqrt(jnp.mean(xc * xc, axis=-1, keepdims=True) + EPS) * g + b


def _sigmoid(x):
    return 1.0 / (1.0 + jnp.exp(-x))


def _swiglu_residual(x1, g, wg_ref, wu_ref, wd_ref):
    h = _rms(x1, g).astype(BF16)
    gate = jnp.dot(h, wg_ref[...], preferred_element_type=F32)
    up = jnp.dot(h, wu_ref[...], preferred_element_type=F32)
    act = (gate * _sigmoid(gate) * up).astype(BF16)
    return x1 + jnp.dot(act, wd_ref[...], preferred_element_type=F32)


def _to_slabs(ref, row0, val):
    for j in range(val.shape[1] // LANES):
        ref[j, row0:row0 + val.shape[0], :] = val[:, j * LANES:(j + 1) * LANES]


def _front_ab_kernel(x_ref, g_ref, w_ref, sgu_g_ref, sgu_b_ref, sgu_w_ref, sgu_bias_ref,
                     qg_ref, kg_ref, rope0_ref, rope1_ref, rope2_ref, headmean_ref,
                     a_ref, qkv0_ref, qkv1_ref, qkv2_ref, stage_ref, stage2_ref):
    tm, d = x_ref.shape
    nchunk = tm // CHUNK
    nslab = d // LANES
    xn = _rms(x_ref[...], g_ref[...])
    h = xn.astype(BF16)

    pa = jnp.dot(h, w_ref[:, 0:2 * A_WIDTH], preferred_element_type=F32)
    act = 0.5 * pa * (1.0 + lax.erf(pa * SQRT_HALF))
    u = act[:, :A_WIDTH]
    vn = _layer_norm(act[:, A_WIDTH:], sgu_g_ref[...], sgu_b_ref[...]).astype(BF16)
    mixed = []
    for g in range(A_GROUPS):
        rhs = jnp.concatenate(
            [vn[c * CHUNK:(c + 1) * CHUNK, g * LANES:(g + 1) * LANES] for c in range(nchunk)], axis=1)
        mixed.append(jnp.dot(sgu_w_ref[g], rhs, preferred_element_type=F32))
    mixed = jnp.concatenate(
        [jnp.concatenate([mixed[g][:, c * LANES:(c + 1) * LANES] for g in range(A_GROUPS)], axis=1)
         for c in range(nchunk)], axis=0)
    bias = jnp.concatenate([sgu_bias_ref[...]] * nchunk, axis=0)
    a_ref[...] = (u * (mixed + bias)).astype(BF16)

    _to_slabs(stage_ref, 0, xn)
    x4 = jnp.concatenate(
        [jnp.concatenate([stage_ref[j, pl.ds(c, tm // 4, stride=4), :] for c in range(4)], axis=0)
         for j in range(nslab)], axis=1)
    _to_slabs(stage2_ref, 0, x4)
    x16 = jnp.concatenate(
        [jnp.concatenate([stage2_ref[j, pl.ds((c % 4) * (tm // 4) + c // 4, tm // 16, stride=4), :]
                          for c in range(16)], axis=0)
         for j in range(nslab)], axis=1)
    h_by_rate = {1: h, 4: x4.astype(BF16), 16: x16.astype(BF16)}

    def rope(y, tab_ref):
        cos, sin_lo, sin_hi = tab_ref[0], tab_ref[1], tab_ref[2]
        parts = []
        for j in range(B_WIDTH // LANES):
            yj = y[:, j * LANES:(j + 1) * LANES]
            nxt = pltpu.roll(yj, LANES - ROPE_DIM // 2, axis=1)
            prv = pltpu.roll(yj, ROPE_DIM // 2, axis=1)
            parts.append(yj * cos + nxt * sin_lo + prv * sin_hi)
        return jnp.concatenate(parts, axis=1)

    def store(out_ref, rate, piece, y):
        cols = slice(piece * B_WIDTH, (piece + 1) * B_WIDTH)
        rows = tm // rate
        for c in range(rate):
            out_ref[c, :, cols] = y[c * rows:(c + 1) * rows, :].astype(BF16)

    base = 2 * A_WIDTH
    groups = list(zip(range(N_DIL), DIL_RATES, (qkv0_ref, qkv1_ref, qkv2_ref), (rope0_ref, rope1_ref, rope2_ref)))
    for grp, rate, out_ref, tab_ref in reversed(groups):
        hg = h_by_rate[rate]
        for piece, gain_ref in ((0, qg_ref), (1, kg_ref)):
            col0 = base + (piece * N_DIL + grp) * B_WIDTH
            p = jnp.dot(hg, w_ref[:, col0:col0 + B_WIDTH], preferred_element_type=F32)
            ms = jnp.dot((p * p).astype(BF16), headmean_ref[...], preferred_element_type=F32)
            y = rope(p * lax.rsqrt(ms + EPS) * gain_ref[grp], tab_ref)
            if piece == 0:
                y = y * Q_SCALE
            store(out_ref, rate, piece, y)
        col0 = base + (2 * N_DIL + grp) * B_WIDTH
        store(out_ref, rate, 2, jnp.dot(hg, w_ref[:, col0:col0 + B_WIDTH], preferred_element_type=F32))


def _front_ab(x, g, w_in, sgu_g, sgu_b, sgu_w, sgu_bias, q_gain, k_gain, ropes, headmean, tm):
    bsz, seq, d = x.shape
    nt = seq // tm
    tile = lambda b, i: (b, i, 0)
    row = lambda b, i: (0, i, 0)
    qkv_shapes = [jax.ShapeDtypeStruct((bsz, r, seq // r, 3 * B_WIDTH), BF16) for r in DIL_RATES]
    qkv_specs = [pl.BlockSpec((None, r, tm // r, 3 * B_WIDTH), lambda b, i: (b, 0, i, 0)) for r in DIL_RATES]
    return pl.pallas_call(
        _front_ab_kernel,
        grid=(bsz, nt),
        in_specs=[
            pl.BlockSpec((None, tm, d), tile),
            _const_spec(g.shape), _const_spec(w_in.shape),
            _const_spec(sgu_g.shape), _const_spec(sgu_b.shape), _const_spec(sgu_w.shape),
            _const_spec(sgu_bias.shape), _const_spec(q_gain.shape), _const_spec(k_gain.shape),
            pl.BlockSpec((3, tm, LANES), row), pl.BlockSpec((3, tm, LANES), row), pl.BlockSpec((3, tm, LANES), row),
            _const_spec(headmean.shape),
        ],
        out_specs=[pl.BlockSpec((None, tm, A_WIDTH), tile)] + qkv_specs,
        out_shape=[jax.ShapeDtypeStruct((bsz, seq, A_WIDTH), BF16)] + qkv_shapes,
        scratch_shapes=[pltpu.VMEM((d // LANES, tm, LANES), F32), pltpu.VMEM((d // LANES, tm, LANES), F32)],
        compiler_params=pltpu.CompilerParams(
            dimension_semantics=("arbitrary", "arbitrary"), vmem_limit_bytes=VMEM_LIMIT),
        name="front_ab",
    )(x, g, w_in, sgu_g, sgu_b, sgu_w, sgu_bias, q_gain, k_gain, *ropes, headmean)


def _dil_attn_kernel(qkv_ref, bias_ref, o_ref, lse_ref, k_ref, v_ref):
    qb = qkv_ref.shape[0]
    j = pl.program_id(2)

    @pl.when(j == 0)
    def _():
        k_ref[0:N_BACK, :] = jnp.zeros((N_BACK, B_WIDTH), BF16)
        v_ref[0:N_BACK, :] = jnp.zeros((N_BACK, B_WIDTH), BF16)

    k_ref[N_BACK:N_BACK + qb, :] = qkv_ref[:, B_WIDTH:2 * B_WIDTH]
    v_ref[N_BACK:N_BACK + qb, :] = qkv_ref[:, 2 * B_WIDTH:3 * B_WIDTH]

    lane = lax.broadcasted_iota(jnp.int32, (N_BACK, LANES), 1)
    low_head = lane < HEAD_DIM
    has_prev = jnp.where(j == 0, 0, 1)

    for s in range(qb // N_BACK):
        rows = slice(s * N_BACK, (s + 1) * N_BACK)
        bias = bias_ref[1] if s > 0 else bias_ref[has_prev]
        for hp in range(B_WIDTH // LANES):
            cols = slice(hp * LANES, (hp + 1) * LANES)
            q2 = qkv_ref[rows, cols]
            zero = jnp.zeros_like(q2)
            qs = jnp.concatenate([jnp.where(low_head, q2, zero), jnp.where(low_head, zero, q2)], axis=0)
            k2 = k_ref[s * N_BACK:(s + 2) * N_BACK, cols]
            v2 = v_ref[s * N_BACK:(s + 2) * N_BACK, cols]
            sc = lax.dot_general(qs, k2, (((1,), (1,)), ((), ())), preferred_element_type=F32) + bias
            m = jnp.max(sc, axis=-1, keepdims=True)
            p = jnp.exp2(sc - m)
            l = jnp.sum(p, axis=-1, keepdims=True)
            pv = jnp.dot(p.astype(BF16), v2, preferred_element_type=F32) / l
            lse = (m + jnp.log2(l)) * LN_2
            o_ref[rows, cols] = jnp.where(low_head, pv[:N_BACK], pv[N_BACK:]).astype(BF16)
            lse_ref[rows, cols] = jnp.where(low_head, lse[:N_BACK], lse[N_BACK:])

    k_ref[0:N_BACK, :] = k_ref[qb:qb + N_BACK, :]
    v_ref[0:N_BACK, :] = v_ref[qb:qb + N_BACK, :]


def _dil_attn(qkv, bias, qb):
    bsz, rate, length, _ = qkv.shape
    blk = lambda b, c, j: (b, c, j, 0)
    return pl.pallas_call(
        _dil_attn_kernel,
        grid=(bsz, rate, length // qb),
        in_specs=[pl.BlockSpec((None, None, qb, 3 * B_WIDTH), blk), _const_spec(bias.shape)],
        out_specs=[pl.BlockSpec((None, None, qb, B_WIDTH), blk), pl.BlockSpec((None, None, qb, B_WIDTH), blk)],
        out_shape=[jax.ShapeDtypeStruct((bsz, rate, length, B_WIDTH), BF16),
                   jax.ShapeDtypeStruct((bsz, rate, length, B_WIDTH), F32)],
        scratch_shapes=[pltpu.VMEM((N_BACK + qb, B_WIDTH), BF16), pltpu.VMEM((N_BACK + qb, B_WIDTH), BF16)],
        compiler_params=pltpu.CompilerParams(
            dimension_semantics=("arbitrary", "arbitrary", "arbitrary"), vmem_limit_bytes=VMEM_LIMIT),
        name=f"dil_attn_r{rate}",
    )(qkv, bias)


def _ab_out_ffn_kernel(x_ref, a_ref, o0_ref, o1_ref, o2_ref, l0_ref, l1_ref, l2_ref,
                       wout_ref, fg_ref, wg_ref, wu_ref, wd_ref, out_ref, nat_ref):
    tm = x_ref.shape[0]

    def natural(ref, rate):
        if rate == 1:
            return ref[0].astype(F32)
        for c in range(rate):
            val = ref[c].astype(F32)
            for j in range(B_WIDTH // LANES):
                nat_ref[j, pl.ds(c, tm // rate, stride=rate), :] = val[:, j * LANES:(j + 1) * LANES]
        return jnp.concatenate([nat_ref[j] for j in range(B_WIDTH // LANES)], axis=1)

    lse = [natural(r, rate) for r, rate in zip((l0_ref, l1_ref, l2_ref), DIL_RATES)]
    m = jnp.maximum(jnp.maximum(lse[0], lse[1]), lse[2])
    e = [jnp.exp(l - m) for l in lse]
    den = e[0] + e[1] + e[2]
    num = e[0] * natural(o0_ref, DIL_RATES[0])
    num = num + e[1] * natural(o1_ref, DIL_RATES[1])
    num = num + e[2] * natural(o2_ref, DIL_RATES[2])
    mix = jnp.concatenate([a_ref[...], (num / den).astype(BF16)], axis=1)
    x1 = x_ref[...] + jnp.dot(mix, wout_ref[...], preferred_element_type=F32)
    out_ref[...] = _swiglu_residual(x1, fg_ref[...], wg_ref, wu_ref, wd_ref)


def _ab_out_ffn(x, a, outs, lses, w_out, fg, wg, wu, wd, tm):
    bsz, seq, d = x.shape
    tile = lambda b, i: (b, i, 0)
    cls = lambda b, i: (b, 0, i, 0)
    cls_specs = [pl.BlockSpec((None, r, tm // r, B_WIDTH), cls) for r in DIL_RATES]
    return pl.pallas_call(
        _ab_out_ffn_kernel,
        grid=(bsz, seq // tm),
        in_specs=[pl.BlockSpec((None, tm, d), tile), pl.BlockSpec((None, tm, A_WIDTH), tile)]
        + cls_specs + cls_specs
        + [_const_spec(w_out.shape), _const_spec(fg.shape), _const_spec(wg.shape),
           _const_spec(wu.shape), _const_spec(wd.shape)],
        out_specs=pl.BlockSpec((None, tm, d), tile),
        out_shape=jax.ShapeDtypeStruct(x.shape, F32),
        scratch_shapes=[pltpu.VMEM((B_WIDTH // LANES, tm, LANES), F32)],
        compiler_params=pltpu.CompilerParams(
            dimension_semantics=("arbitrary", "arbitrary"), vmem_limit_bytes=VMEM_LIMIT),
        name="ab_out_ffn",
    )(x, a, *outs, *lses, w_out, fg, wg, wu, wd)


def _causal_conv(hist_ref, halo, taps, w_ref, tm):
    cols = []
    for j in range(hist_ref.shape[0]):
        acc = None
        for k in range(taps):
            off = halo - (taps - 1) + k
            term = hist_ref[j, pl.ds(off, tm, stride=1), :] * w_ref[k:k + 1, j * LANES:(j + 1) * LANES]
            acc = term if acc is None else acc + term
        cols.append(acc)
    return jnp.concatenate(cols, axis=1)


def _layer_cd_kernel(x_ref, g_ref, win_ref, cw_ref, cb_ref, lng_ref, lnb_ref, dw_ref,
                     wout_ref, fg_ref, wg_ref, wu_ref, wd_ref, out_ref, chist_ref, dhist_ref):
    tm = x_ref.shape[0]

    @pl.when(pl.program_id(1) == 0)
    def _():
        chist_ref[:, 0:C_HALO, :] = jnp.zeros((C_WIDTH // LANES, C_HALO, LANES), F32)
        dhist_ref[:, 0:D_HALO, :] = jnp.zeros((D_WIDTH // LANES, D_HALO, LANES), F32)

    x = x_ref[...]
    h = _rms(x, g_ref[...]).astype(BF16)

    pc = jnp.dot(h, win_ref[:, 0:2 * C_WIDTH], preferred_element_type=F32)
    _to_slabs(chist_ref, C_HALO, pc[:, :C_WIDTH] * _sigmoid(pc[:, C_WIDTH:]))
    conv_c = _causal_conv(chist_ref, C_HALO, C_KERNEL, cw_ref, tm) + cb_ref[...]
    chist_ref[:, 0:C_HALO, :] = chist_ref[:, tm:tm + C_HALO, :]
    cn = _layer_norm(conv_c, lng_ref[...], lnb_ref[...])
    c_out = (cn * _sigmoid(cn)).astype(BF16)

    pd = jnp.dot(h, win_ref[:, 2 * C_WIDTH:], preferred_element_type=F32)
    _to_slabs(dhist_ref, D_HALO, pd[:, D_WIDTH:2 * D_WIDTH] * pd[:, 2 * D_WIDTH:])
    conv_d = _causal_conv(dhist_ref, D_HALO, D_KERNEL, dw_ref, tm)
    dhist_ref[:, 0:D_HALO, :] = dhist_ref[:, tm:tm + D_HALO, :]
    d_out = (pd[:, :D_WIDTH] * conv_d).astype(BF16)

    mix = jnp.concatenate([c_out, d_out], axis=1)
    x1 = x + jnp.dot(mix, wout_ref[...], preferred_element_type=F32)
    out_ref[...] = _swiglu_residual(x1, fg_ref[...], wg_ref, wu_ref, wd_ref)


def _layer_cd(x, g, w_in, cw, cb, lng, lnb, dw, w_out, fg, wg, wu, wd, tm):
    bsz, seq, d = x.shape
    tile = lambda b, i: (b, i, 0)
    consts = (g, w_in, cw, cb, lng, lnb, dw, w_out, fg, wg, wu, wd)
    return pl.pallas_call(
        _layer_cd_kernel,
        grid=(bsz, seq // tm),
        in_specs=[pl.BlockSpec((None, tm, d), tile)] + [_const_spec(c.shape) for c in consts],
        out_specs=pl.BlockSpec((None, tm, d), tile),
        out_shape=jax.ShapeDtypeStruct(x.shape, F32),
        scratch_shapes=[pltpu.VMEM((C_WIDTH // LANES, C_HALO + tm, LANES), F32),
                        pltpu.VMEM((D_WIDTH // LANES, D_HALO + tm, LANES), F32)],
        compiler_params=pltpu.CompilerParams(
            dimension_semantics=("arbitrary", "arbitrary"), vmem_limit_bytes=VMEM_LIMIT),
        name="layer_cd",
    )(x, *consts)


def _rope_tables(seq, tm):
    half = ROPE_DIM // 2
    pos = jnp.arange(seq, dtype=F32)
    inv_freq = ROPE_THETA ** (-jnp.arange(half, dtype=F32) * 2.0 / ROPE_DIM)
    ang = pos[:, None] * inv_freq[None, :]
    cos, sin = jnp.cos(ang), jnp.sin(ang)
    ones = jnp.ones((seq, HEAD_DIM - ROPE_DIM), F32)
    zeros = jnp.zeros((seq, HEAD_DIM - ROPE_DIM), F32)
    zh = jnp.zeros((seq, half), F32)
    cos_t = jnp.concatenate([cos, cos, ones], axis=1)
    sin_lo = jnp.concatenate([-sin, zh, zeros], axis=1)
    sin_hi = jnp.concatenate([zh, sin, zeros], axis=1)
    nat = jnp.stack([jnp.concatenate([t, t], axis=1) for t in (cos_t, sin_lo, sin_hi)])
    tabs = []
    for r in DIL_RATES:
        t = nat.reshape(3, seq // tm, tm // r, r, LANES).transpose(0, 1, 3, 2, 4)
        tabs.append(t.reshape(3, seq, LANES))
    return tabs


def _attn_bias():
    qi = np.arange(N_BACK)[:, None]
    kj = np.arange(2 * N_BACK)[None, :]
    dist = qi + N_BACK - kj
    band = (dist >= 0) & (dist <= N_BACK)
    masks = np.stack([band & (kj >= N_BACK), band])
    masks = np.concatenate([masks, masks], axis=1)
    return jnp.asarray(np.where(masks, 0.0, NEG_INF), dtype=F32)


def kernel(x, ab_norm_g, ab_w_in, sgu_norm_g, sgu_norm_b, sgu_w, sgu_bias, q_norm_g, k_norm_g, ab_w_out,
           cd_norm_g, cd_w_in, conv_c_w, conv_c_b, c_ln_g, c_ln_b, conv_d_w, cd_w_out,
           ffn_norm_g, ffn_w_gate, ffn_w_up, ffn_w_down):
    bsz, seq, d = x.shape
    tm = TOKEN_TILE
    row = lambda v: v.reshape(1, -1)

    causal = jnp.tril(jnp.ones((CHUNK, CHUNK), dtype=bool))
    sgu_w_m = jnp.where(causal[None], sgu_w[0], 0.0).astype(BF16)
    sgu_bias_t = jnp.repeat(sgu_bias[0].T, A_WIDTH // A_GROUPS, axis=1)
    tile_heads = lambda gain: jnp.tile(gain, (1, B_WIDTH // HEAD_DIM)).reshape(N_DIL, 1, B_WIDTH)
    head_id = np.arange(B_WIDTH) // HEAD_DIM
    headmean = jnp.asarray((head_id[:, None] == head_id[None, :]) / HEAD_DIM, dtype=BF16)

    a_out, qkv0, qkv1, qkv2 = _front_ab(
        x, row(ab_norm_g[0]), ab_w_in[0].astype(BF16), row(sgu_norm_g[0]), row(sgu_norm_b[0]), sgu_w_m,
        sgu_bias_t, tile_heads(q_norm_g[0]), tile_heads(k_norm_g[0]), _rope_tables(seq, tm), headmean, tm)

    bias = _attn_bias()
    outs, lses = [], []
    for qkv in (qkv0, qkv1, qkv2):
        o, l = _dil_attn(qkv, bias, min(ATTN_Q_ROWS, qkv.shape[2]))
        outs.append(o)
        lses.append(l)

    x = _ab_out_ffn(x, a_out, outs, lses, ab_w_out[0].astype(BF16), row(ffn_norm_g[0]),
                    ffn_w_gate[0].astype(BF16), ffn_w_up[0].astype(BF16), ffn_w_down[0].astype(BF16), tm)

    x = _layer_cd(x, row(cd_norm_g[0]), cd_w_in[0].astype(BF16), conv_c_w[0], row(conv_c_b[0]),
                  row(c_ln_g[0]), row(c_ln_b[0]), conv_d_w[0], cd_w_out[0].astype(BF16),
                  row(ffn_norm_g[1]), ffn_w_gate[1].astype(BF16), ffn_w_up[1].astype(BF16),
                  ffn_w_down[1].astype(BF16), tm)
    return x
```
